```python
import math
import jax, jax.numpy as jnp
from jax import lax
import numpy as np

D_MODEL = 2048
BATCH = 2
SEQ = 4096
DEPTH = 2

CHUNK = 64
N_A = DEPTH // 2
N_B = DEPTH - N_A
N_SUB = 3
D_FF = 3 * D_MODEL
D_RNN = D_MODEL
LRU_BLOCK = 256
LRU_HEADS = D_RNN // LRU_BLOCK
CONV_WIDTH = 4
LRU_C = 8.0
N_HEADS = 8
HEAD_DIM = D_MODEL // (2 * N_HEADS)
V_DIM = 2 * HEAD_DIM
Q_BLOCK = 128
NORM_EPS = 1e-6

kernel_name = "hybrid_rglru_diffattn_yoco_trunk"


def rmsnorm(x, g):
    xf = x.astype(jnp.float32)
    y = xf * lax.rsqrt(jnp.mean(xf * xf, axis=-1, keepdims=True) + NORM_EPS)
    return (y * g.astype(jnp.float32)).astype(x.dtype)


def swiglu(h, w_in, w_out):
    g, u = jnp.split(h @ w_in, 2, axis=-1)
    return (jax.nn.silu(g) * u) @ w_out


def modulated_sublayer(x, fn, g_pre, g_post, mod_j, weight):
    shift = mod_j[:, None, 0]
    scale = mod_j[:, None, 1]
    gate = mod_j[:, None, 2]
    h = rmsnorm(x, g_pre) * (1.0 + scale) + shift
    return x + weight * gate * rmsnorm(fn(h), g_post)


def causal_depthwise_conv(x, w, b):
    S = x.shape[1]
    xp = jnp.pad(x, ((0, 0), (CONV_WIDTH - 1, 0), (0, 0)))
    y = b
    for k in range(CONV_WIDTH):
        y = y + xp[:, k:k + S] * w[k]
    return y


def rglru(x, w_a, b_a, w_x, b_x, lam):
    B, S, C = x.shape
    xb = x.reshape(B, S, LRU_HEADS, LRU_BLOCK)
    r = jax.nn.sigmoid(jnp.einsum('bshi,hij->bshj', xb, w_a).reshape(B, S, C) + b_a)
    i = jax.nn.sigmoid(jnp.einsum('bshi,hij->bshj', xb, w_x).reshape(B, S, C) + b_x)
    log_a = -LRU_C * r.astype(jnp.float32) * jax.nn.softplus(-lam.astype(jnp.float32))
    a = jnp.exp(log_a)
    u = jnp.sqrt(-jnp.expm1(2.0 * log_a)) * (i * x).astype(jnp.float32)

    def combine(left, right):
        a_l, h_l = left
        a_r, h_r = right
        return a_l * a_r, a_r * h_l + h_r

    _, h = lax.associative_scan(combine, (a, u), axis=1)
    return h.astype(x.dtype)


def recurrent_mixer(h, w_in, conv_w, conv_b, w_a, b_a, w_x, b_x, lam, w_out):
    y, xr = jnp.split(h @ w_in, 2, axis=-1)
    xr = causal_depthwise_conv(xr, conv_w, conv_b)
    xr = rglru(xr, w_a, b_a, w_x, b_x, lam)
    return (jax.nn.gelu(y) * xr) @ w_out


def shared_kv(x, c_act, g_kv, w_kv_mod, b_kv_mod, w_kv):
    B, S, _ = x.shape
    shift, scale = jnp.split(c_act @ w_kv_mod + b_kv_mod, 2, axis=-1)
    h = rmsnorm(x, g_kv) * (1.0 + scale[:, None]) + shift[:, None]
    k, v = jnp.split(h @ w_kv, 2, axis=-1)
    return (k.reshape(B, S, N_HEADS, 2, HEAD_DIM), v.reshape(B, S, N_HEADS, V_DIM))


def diff_attention(h, k, v, w_q, lam_qk, g_sub, w_o, lam_init):
    B, S, _ = h.shape
    q = (h @ w_q).reshape(B, S, N_HEADS, 2, HEAD_DIM)
    lq = lam_qk.astype(jnp.float32)
    lam = jnp.exp(jnp.sum(lq[0] * lq[1])) - jnp.exp(jnp.sum(lq[2] * lq[3])) + lam_init
    key_chunk = jnp.arange(S) // CHUNK
    scale = HEAD_DIM ** -0.5
    vf = v.astype(jnp.float32)

    def block(i):
        q0 = i * Q_BLOCK
        qb = lax.dynamic_slice_in_dim(q, q0, Q_BLOCK, axis=1)
        s = jnp.einsum('bqhcd,bkhcd->bhcqk', qb, k).astype(jnp.float32) * scale
        q_chunk = (q0 + jnp.arange(Q_BLOCK)) // CHUNK
        mask = key_chunk[None, :] <= q_chunk[:, None]
        p = jax.nn.softmax(jnp.where(mask, s, -jnp.inf), axis=-1)
        attn = p[:, :, 0] - lam * p[:, :, 1]
        return jnp.einsum('bhqk,bkhe->bqhe', attn, vf)

    o = lax.map(block, jnp.arange(S // Q_BLOCK))
    o = jnp.moveaxis(o, 0, 1).reshape(B, S, N_HEADS, V_DIM)
    o = rmsnorm(o, g_sub) * (1.0 - lam_init)
    return o.reshape(B, S, N_HEADS * V_DIM).astype(h.dtype) @ w_o


def setup_inputs(seed: int = 0) -> dict:
    key = jax.random.key(seed)
    ks = iter(jax.random.split(key, 32))
    f32 = jnp.float32
    D = D_MODEL

    def nrm(shape, std):
        return jax.random.normal(next(ks), shape, f32) * std

    u = jax.random.uniform(next(ks), (N_A, D_RNN), f32, 0.9, 0.999)
    a0 = u ** (1.0 / LRU_C)
    lru_lambda = jnp.log(a0) - jnp.log1p(-a0)
    return {
        'x': nrm((BATCH, SEQ, D), 1.0),
        'c': nrm((BATCH, D), 1.0),
        'w_mod': nrm((DEPTH, D, 3 * N_SUB * D), 0.5 * D ** -0.5),
        'b_mod': nrm((DEPTH, 3 * N_SUB * D), 0.02),
        'norm_gain': 1.0 + nrm((DEPTH, 2 * N_SUB, D), 0.02),
        'w_ffn_in': nrm((DEPTH, 2, D, 2 * D_FF), D ** -0.5),
        'w_ffn_out': nrm((DEPTH, 2, D_FF, D), D_FF ** -0.5),
        'lru_w_in': nrm((N_A, D, 2 * D_RNN), D ** -0.5),
        'lru_conv_w': nrm((N_A, CONV_WIDTH, D_RNN), CONV_WIDTH ** -0.5),
        'lru_conv_b': nrm((N_A, D_RNN), 0.02),
        'lru_w_a': nrm((N_A, LRU_HEADS, LRU_BLOCK, LRU_BLOCK), LRU_BLOCK ** -0.5),
        'lru_b_a': nrm((N_A, D_RNN), 0.02),
        'lru_w_x': nrm((N_A, LRU_HEADS, LRU_BLOCK, LRU_BLOCK), LRU_BLOCK ** -0.5),
        'lru_b_x': nrm((N_A, D_RNN), 0.02),
        'lru_lambda': lru_lambda,
        'lru_w_out': nrm((N_A, D_RNN, D), D_RNN ** -0.5),
        'kv_gain': 1.0 + nrm((D,), 0.02),
        'kv_w_mod': nrm((D, 2 * D), 0.5 * D ** -0.5),
        'kv_b_mod': nrm((2 * D,), 0.02),
        'w_kv': nrm((D, N_HEADS * 2 * HEAD_DIM + N_HEADS * V_DIM), D ** -0.5),
        'attn_w_q': nrm((N_B, D, N_HEADS * 2 * HEAD_DIM), D ** -0.5),
        'attn_lambda_qk': nrm((N_B, 4, HEAD_DIM), 0.1),
        'attn_sub_gain': 1.0 + nrm((N_B, V_DIM), 0.02),
        'attn_w_o': nrm((N_B, N_HEADS * V_DIM, D), (N_HEADS * V_DIM) ** -0.5),
    }


def reference(x, c, w_mod, b_mod, norm_gain, w_ffn_in, w_ffn_out,
              lru_w_in, lru_conv_w, lru_conv_b, lru_w_a, lru_b_a, lru_w_x, lru_b_x,
              lru_lambda, lru_w_out, kv_gain, kv_w_mod, kv_b_mod, w_kv,
              attn_w_q, attn_lambda_qk, attn_sub_gain, attn_w_o):
    B = x.shape[0]
    c_act = jax.nn.silu(c)
    k = None
    v = None
    for l in range(DEPTH):
        if l == N_A:
            k, v = shared_kv(x, c_act, kv_gain, kv_w_mod, kv_b_mod, w_kv)
        mod = (c_act @ w_mod[l] + b_mod[l]).reshape(B, N_SUB, 3, D_MODEL)
        g = norm_gain[l]
        x = modulated_sublayer(x, lambda h: swiglu(h, w_ffn_in[l, 0], w_ffn_out[l, 0]),
                               g[0], g[1], mod[:, 0], 0.5)
        if l < N_A:
            j = l
            mixer = lambda h: recurrent_mixer(h, lru_w_in[j], lru_conv_w[j], lru_conv_b[j],
                                              lru_w_a[j], lru_b_a[j], lru_w_x[j], lru_b_x[j],
                                              lru_lambda[j], lru_w_out[j])
        else:
            j = l - N_A
            lam_init = 0.8 - 0.6 * math.exp(-0.3 * l)
            mixer = lambda h: diff_attention(h, k, v, attn_w_q[j], attn_lambda_qk[j],
                                             attn_sub_gain[j], attn_w_o[j], lam_init)
        x = modulated_sublayer(x, mixer, g[2], g[3], mod[:, 1], 1.0)
        x = modulated_sublayer(x, lambda h: swiglu(h, w_ffn_in[l, 1], w_ffn_out[l, 1]),
                               g[4], g[5], mod[:, 2], 0.5)
    return x
```

```python
import functools
import math

import jax
import jax.numpy as jnp
from jax import lax
from jax.experimental import pallas as pl
from jax.experimental.pallas import tpu as pltpu

F32 = jnp.float32
BF16 = jnp.bfloat16

NORM_EPS = 1e-6
CHUNK = 64
LRU_C = 8.0
N_HEADS = 8
N_SUB = 3
MASK_VALUE = -1e30

V7X_VMEM_BYTES = 64 * 1024 * 1024
VMEM_LIMIT_BYTES = V7X_VMEM_BYTES - 8 * 1024 * 1024
SUBLANES = 8
ROW_CHUNK = 64


def _params(*sem):
    return pltpu.CompilerParams(dimension_semantics=sem, vmem_limit_bytes=VMEM_LIMIT_BYTES)


def _rms(x):
    return x * lax.rsqrt(jnp.mean(x * x, axis=-1, keepdims=True) + NORM_EPS)


def _dot(a, b):
    return jnp.dot(a, b, preferred_element_type=F32)


def _row_chunks(n_rows, fn):
    rc = min(ROW_CHUNK, n_rows)

    def body(r, carry):
        fn(pl.ds(pl.multiple_of(r * rc, rc), rc))
        return carry

    lax.fori_loop(0, n_rows // rc, body, 0)


def _modulated_norm_to(h_ref, x_ref, mod_ref, g_ref):
    shift = mod_ref[0:1, :]
    scale1 = 1.0 + mod_ref[1:2, :]
    g_pre = g_ref[0:1, :]

    def fn(rows):
        h = _rms(x_ref[rows, :]) * g_pre * scale1 + shift
        h_ref[rows, :] = h.astype(BF16)

    _row_chunks(x_ref.shape[0], fn)


def _residual_epilogue(o_ref, x_ref, gate, g_post, weight):
    wg = weight * gate

    def fn(rows):
        o_ref[rows, :] = x_ref[rows, :] + wg * (_rms(o_ref[rows, :]) * g_post)

    _row_chunks(o_ref.shape[0], fn)


def _modvec_kernel(c_ref, w_ref, b_ref, o_ref):
    c = c_ref[...]
    c_act = (c * jax.nn.sigmoid(c)).astype(BF16)
    o_ref[...] = _dot(c_act, w_ref[...].astype(BF16)) + b_ref[...]


def _modvec(c_pad, w, b, tn=1024):
    nl, d, n = w.shape
    tn = min(tn, n)
    return pl.pallas_call(
        _modvec_kernel,
        grid=(nl, n // tn),
        in_specs=[
            pl.BlockSpec((SUBLANES, d), lambda l, j: (0, 0)),
            pl.BlockSpec((None, d, tn), lambda l, j: (l, 0, j)),
            pl.BlockSpec((None, 1, tn), lambda l, j: (l, 0, j)),
        ],
        out_specs=pl.BlockSpec((None, SUBLANES, tn), lambda l, j: (l, 0, j)),
        out_shape=jax.ShapeDtypeStruct((nl, SUBLANES, n), F32),
        compiler_params=_params("arbitrary", "arbitrary"),
        name="modvec",
    )(c_pad, w, b.reshape(nl, 1, n))


def _ffn_kernel(x_ref, mod_ref, g_ref, wg_ref, wu_ref, wo_ref, o_ref, h_ref):
    j = pl.program_id(1)

    @pl.when(j == 0)
    def _():
        _modulated_norm_to(h_ref, x_ref, mod_ref, g_ref)
        o_ref[...] = jnp.zeros(o_ref.shape, F32)

    h = h_ref[...]
    g = _dot(h, wg_ref[...].astype(BF16))
    u = _dot(h, wu_ref[...].astype(BF16))
    act = (g * jax.nn.sigmoid(g) * u).astype(BF16)
    o_ref[...] += _dot(act, wo_ref[...].astype(BF16))

    @pl.when(j == pl.num_programs(1) - 1)
    def _():
        _residual_epilogue(o_ref, x_ref, mod_ref[2:3, :], g_ref[1:2, :], 0.5)


def _ffn(x, mod, gains, w_in, w_out, layer, half, sub, seq, tm=1024, tf=256):
    t, d = x.shape
    f = w_out.shape[2]
    tm = min(tm, seq)
    tf = min(tf, f)
    per_batch = seq // tm
    nf = f // tf
    return pl.pallas_call(
        _ffn_kernel,
        grid=(t // tm, nf),
        in_specs=[
            pl.BlockSpec((tm, d), lambda i, j: (i, 0)),
            pl.BlockSpec((None, None, 3, d), lambda i, j: (i // per_batch, sub, 0, 0)),
            pl.BlockSpec((None, None, 2, d), lambda i, j: (layer, sub, 0, 0)),
            pl.BlockSpec((None, None, d, tf), lambda i, j: (layer, half, 0, j)),
            pl.BlockSpec((None, None, d, tf), lambda i, j: (layer, half, 0, j + nf)),
            pl.BlockSpec((None, None, tf, d), lambda i, j: (layer, half, j, 0)),
        ],
        out_specs=pl.BlockSpec((tm, d), lambda i, j: (i, 0)),
        out_shape=jax.ShapeDtypeStruct((t, d), F32),
        scratch_shapes=[pltpu.VMEM((tm, d), BF16)],
        compiler_params=_params("arbitrary", "arbitrary"),
        name=f"ffn_l{layer}h{half}",
    )(x, mod, gains, w_in, w_in, w_out)


def _proj_kernel(x_ref, mod_ref, g_ref, w_ref, o_ref, h_ref):
    @pl.when(pl.program_id(1) == 0)
    def _():
        _modulated_norm_to(h_ref, x_ref, mod_ref, g_ref)

    o_ref[...] = _dot(h_ref[...], w_ref[...].astype(BF16)).astype(o_ref.dtype)


def _proj(x, mod, mod_idx, gains, gain_idx, w, seq, out_dtype, name, tm=1024, tn=1024):
    t, d = x.shape
    n = w.shape[1]
    tm = min(tm, seq)
    tn = min(tn, n)
    per_batch = seq // tm
    mr, gr = mod.shape[2], gains.shape[2]
    gi0, gi1 = gain_idx
    return pl.pallas_call(
        _proj_kernel,
        grid=(t // tm, n // tn),
        in_specs=[
            pl.BlockSpec((tm, d), lambda i, j: (i, 0)),
            pl.BlockSpec((None, None, mr, d), lambda i, j: (i // per_batch, mod_idx, 0, 0)),
            pl.BlockSpec((None, None, gr, d), lambda i, j: (gi0, gi1, 0, 0)),
            pl.BlockSpec((d, tn), lambda i, j: (0, j)),
        ],
        out_specs=pl.BlockSpec((tm, tn), lambda i, j: (i, j)),
        out_shape=jax.ShapeDtypeStruct((t, n), out_dtype),
        scratch_shapes=[pltpu.VMEM((tm, d), BF16)],
        compiler_params=_params("arbitrary", "arbitrary"),
        name=name,
    )(x, mod, gains, w)


def _outproj_kernel(a_ref, x_ref, mod_ref, g_ref, w_ref, o_ref):
    k = pl.program_id(1)

    @pl.when(k == 0)
    def _():
        o_ref[...] = jnp.zeros(o_ref.shape, F32)

    o_ref[...] += _dot(a_ref[...], w_ref[...].astype(BF16))

    @pl.when(k == pl.num_programs(1) - 1)
    def _():
        _residual_epilogue(o_ref, x_ref, mod_ref[2:3, :], g_ref[1:2, :], 1.0)


def _outproj(a, x, mod, gains, w, layer, seq, name, tm=1024, tk=512):
    t, d = x.shape
    kdim = a.shape[1]
    tm = min(tm, seq)
    tk = min(tk, kdim)
    per_batch = seq // tm
    return pl.pallas_call(
        _outproj_kernel,
        grid=(t // tm, kdim // tk),
        in_specs=[
            pl.BlockSpec((tm, tk), lambda i, k: (i, k)),
            pl.BlockSpec((tm, d), lambda i, k: (i, 0)),
            pl.BlockSpec((None, None, 3, d), lambda i, k: (i // per_batch, 1, 0, 0)),
            pl.BlockSpec((None, None, 2, d), lambda i, k: (layer, 1, 0, 0)),
            pl.BlockSpec((tk, d), lambda i, k: (k, 0)),
        ],
        out_specs=pl.BlockSpec((tm, d), lambda i, k: (i, 0)),
        out_shape=jax.ShapeDtypeStruct((t, d), F32),
        compiler_params=_params("arbitrary", "arbitrary"),
        name=name,
    )(a, x, mod, gains, w)


def _gelu_tanh(x):
    cdf = 0.5 * (1.0 + jnp.tanh(math.sqrt(2.0 / math.pi) * (x + 0.044715 * (x * x * x))))
    return x * cdf


def _softplus(z):
    return jnp.maximum(z, 0.0) + jnp.log1p(jnp.exp(-jnp.abs(z)))


def _rglru_kernel(y_ref, xr_ref, cw_ref, cb_ref, wa_ref, wx_ref, ba_ref, bx_ref, lam_ref, z_ref,
                  xbuf, a_s, u_s, carry):
    tc, c = xr_ref.shape
    halo = SUBLANES
    kw = cw_ref.shape[0]

    @pl.when(pl.program_id(2) == 0)
    def _():
        xbuf[0:halo, :] = jnp.zeros((halo, c), F32)
        carry[...] = jnp.zeros((SUBLANES, c), F32)

    xbuf[halo:halo + tc, :] = xr_ref[...]
    xc = cb_ref[...]
    for k in range(kw):
        off = halo - (kw - 1) + k
        xc = xc + xbuf[off:off + tc, :] * cw_ref[k:k + 1, :]
    xbuf[0:halo, :] = xbuf[tc:tc + halo, :]

    xcb = xc.astype(BF16)
    r = jax.nn.sigmoid(_dot(xcb, wa_ref[...].astype(BF16)) + ba_ref[...])
    i = jax.nn.sigmoid(_dot(xcb, wx_ref[...].astype(BF16)) + bx_ref[...])
    log_a = (-LRU_C * r) * _softplus(-lam_ref[...])
    a = jnp.exp(log_a)
    a_s[...] = a
    u_s[...] = jnp.sqrt(-jnp.tanh(log_a) * (a * a + 1.0)) * (i * xc)

    row = lax.broadcasted_iota(jnp.int32, (SUBLANES, c), 0)

    def body(g, h_prev):
        rows = pl.ds(pl.multiple_of(g * SUBLANES, SUBLANES), SUBLANES)
        a = a_s[rows, :]
        u = u_s[rows, :]
        for d in (1, 2, 4):
            keep = row >= d
            u_shift = jnp.where(keep, pltpu.roll(u, d, 0), 0.0)
            a_shift = jnp.where(keep, pltpu.roll(a, d, 0), 1.0)
            u = u + a * u_shift
            a = a * a_shift
        h = u + a * h_prev
        u_s[rows, :] = h
        return jnp.broadcast_to(h[SUBLANES - 1:SUBLANES, :], (SUBLANES, c))

    carry[...] = lax.fori_loop(0, tc // SUBLANES, body, carry[...], unroll=8)
    z_ref[...] = (_gelu_tanh(y_ref[...]) * u_s[...]).astype(BF16)


def _rglru(yx, conv_w, conv_b, w_a, b_a, w_x, b_x, lam, batch, seq, tc=512):
    t = yx.shape[0]
    nh, cb, _ = w_a.shape
    c = nh * cb
    tc = min(tc, seq)
    nt = seq // tc
    kw = conv_w.shape[0]
    row = lambda v: v.reshape(1, c)
    vec_spec = pl.BlockSpec((1, cb), lambda b, h, s: (0, h))
    return pl.pallas_call(
        _rglru_kernel,
        grid=(batch, nh, nt),
        in_specs=[
            pl.BlockSpec((tc, cb), lambda b, h, s: (b * nt + s, h)),
            pl.BlockSpec((tc, cb), lambda b, h, s: (b * nt + s, nh + h)),
            pl.BlockSpec((kw, cb), lambda b, h, s: (0, h)),
            vec_spec,
            pl.BlockSpec((None, cb, cb), lambda b, h, s: (h, 0, 0)),
            pl.BlockSpec((None, cb, cb), lambda b, h, s: (h, 0, 0)),
            vec_spec, vec_spec, vec_spec,
        ],
        out_specs=pl.BlockSpec((tc, cb), lambda b, h, s: (b * nt + s, h)),
        out_shape=jax.ShapeDtypeStruct((t, c), BF16),
        scratch_shapes=[
            pltpu.VMEM((tc + SUBLANES, cb), F32),
            pltpu.VMEM((tc, cb), F32),
            pltpu.VMEM((tc, cb), F32),
            pltpu.VMEM((SUBLANES, cb), F32),
        ],
        compiler_params=_params("arbitrary", "arbitrary", "arbitrary"),
        name="rglru",
    )(yx, yx, conv_w, row(conv_b), w_a, w_x, row(b_a), row(b_x), row(lam))


def _flash_kernel(q_ref, k_ref, v_ref, lam_ref, gs_ref, o_ref, m_s, l_s, acc_s, *, lam_init, tk):
    tq, dq = q_ref.shape
    hd = dq // 2
    dv = v_ref.shape[1]
    scale = hd ** -0.5
    i = pl.program_id(2)

    m_s[...] = jnp.full(m_s.shape, MASK_VALUE, F32)
    l_s[...] = jnp.zeros(l_s.shape, F32)
    acc_s[...] = jnp.zeros(acc_s.shape, F32)

    def step(j, masked):
        rows = pl.ds(pl.multiple_of(j * tk, tk), tk)
        k = k_ref[rows, :]
        v = v_ref[rows, :]
        for c in range(2):
            q_c = q_ref[:, c * hd:(c + 1) * hd]
            k_c = k[:, c * hd:(c + 1) * hd]
            s = lax.dot_general(q_c, k_c, (((1,), (1,)), ((), ())), preferred_element_type=F32) * scale
            if masked:
                q_chunk = (i * tq + lax.broadcasted_iota(jnp.int32, (tq, tk), 0)) // CHUNK
                k_chunk = (j * tk + lax.broadcasted_iota(jnp.int32, (tq, tk), 1)) // CHUNK
                s = jnp.where(k_chunk <= q_chunk, s, MASK_VALUE)
            m_prev = m_s[c]
            m_new = jnp.maximum(m_prev, jnp.max(s, axis=-1, keepdims=True))
            alpha = jnp.exp(m_prev - m_new)
            p = jnp.exp(s - m_new)
            l_s[c] = alpha * l_s[c] + jnp.sum(p, axis=-1, keepdims=True)
            acc_s[c] = alpha * acc_s[c] + _dot(p.astype(BF16), v)
            m_s[c] = m_new

    n_full = (i * tq) // tk
    n_all = ((i + 1) * tq + tk - 1) // tk

    def full_body(j, carry):
        step(j, False)
        return carry

    def masked_body(j, carry):
        step(j, True)
        return carry

    lax.fori_loop(0, n_full, full_body, 0)
    lax.fori_loop(n_full, n_all, masked_body, 0)

    lq = lam_ref[...]
    lam = (jnp.exp(jnp.sum(lq[0:1, :] * lq[1:2, :], axis=-1, keepdims=True))
           - jnp.exp(jnp.sum(lq[2:3, :] * lq[3:4, :], axis=-1, keepdims=True)) + lam_init)
    o = acc_s[0] / l_s[0] - lam * (acc_s[1] / l_s[1])
    o = _rms(o) * gs_ref[...] * (1.0 - lam_init)
    o_ref[...] = o.astype(o_ref.dtype)


def _flash(q, kv, lam_qk, g_sub, lam_init, batch, seq, tq=512, tk=512):
    t, dq_all = q.shape
    dq = dq_all // N_HEADS
    dv = (kv.shape[1] - dq_all) // N_HEADS
    tq = min(tq, seq)
    tk = min(tk, seq)
    nq = seq // tq
    v_off = dq_all // dv
    kernel = functools.partial(_flash_kernel, lam_init=lam_init, tk=tk)
    return pl.pallas_call(
        kernel,
        grid=(batch, N_HEADS, nq),
        in_specs=[
            pl.BlockSpec((tq, dq), lambda b, h, i: (b * nq + i, h)),
            pl.BlockSpec((seq, dq), lambda b, h, i: (b, h)),
            pl.BlockSpec((seq, dv), lambda b, h, i: (b, v_off + h)),
            pl.BlockSpec(lam_qk.shape, lambda b, h, i: (0, 0)),
            pl.BlockSpec((1, dv), lambda b, h, i: (0, 0)),
        ],
        out_specs=pl.BlockSpec((tq, dv), lambda b, h, i: (b * nq + i, h)),
        out_shape=jax.ShapeDtypeStruct((t, N_HEADS * dv), BF16),
        scratch_shapes=[
            pltpu.VMEM((2, tq, 1), F32),
            pltpu.VMEM((2, tq, 1), F32),
            pltpu.VMEM((2, tq, dv), F32),
        ],
        compiler_params=_params("arbitrary", "arbitrary", "arbitrary"),
        name="flash",
    )(q, kv, kv, lam_qk, g_sub.reshape(1, dv))


def kernel(x, c, w_mod, b_mod, norm_gain, w_ffn_in, w_ffn_out, lru_w_in, lru_conv_w, lru_conv_b, lru_w_a, lru_b_a, lru_w_x, lru_b_x, lru_lambda, lru_w_out, kv_gain, kv_w_mod, kv_b_mod, w_kv, attn_w_q, attn_lambda_qk, attn_sub_gain, attn_w_o):
    batch, seq, d = x.shape
    depth = w_mod.shape[0]
    n_a = lru_w_in.shape[0]
    assert batch <= SUBLANES and seq % CHUNK == 0

    c_pad = jnp.pad(c, ((0, SUBLANES - batch), (0, 0)))
    mod = _modvec(c_pad, w_mod, b_mod)[:, :batch].reshape(depth, batch, N_SUB, 3, d)
    kv_mod = _modvec(c_pad, kv_w_mod[None], kv_b_mod[None])[:, :batch].reshape(batch, 1, 2, d)
    gains = norm_gain.reshape(depth, N_SUB, 2, d)
    kv_gains = kv_gain.reshape(1, 1, 1, d)

    xt = x.reshape(batch * seq, d)
    kv = None
    for l in range(depth):
        if l == n_a:
            kv = _proj(xt, kv_mod, 0, kv_gains, (0, 0), w_kv, seq, BF16, "kv_proj")
        xt = _ffn(xt, mod[l], gains, w_ffn_in, w_ffn_out, l, 0, 0, seq)
        if l < n_a:
            yx = _proj(xt, mod[l], 1, gains, (l, 1), lru_w_in[l], seq, F32, "lru_in")
            z = _rglru(yx, lru_conv_w[l], lru_conv_b[l], lru_w_a[l], lru_b_a[l], lru_w_x[l], lru_b_x[l],
                       lru_lambda[l], batch, seq)
            xt = _outproj(z, xt, mod[l], gains, lru_w_out[l], l, seq, "lru_out")
        else:
            j = l - n_a
            lam_init = 0.8 - 0.6 * math.exp(-0.3 * l)
            q = _proj(xt, mod[l], 1, gains, (l, 1), attn_w_q[j], seq, BF16, "q_proj")
            o = _flash(q, kv, attn_lambda_qk[j], attn_sub_gain[j], lam_init, batch, seq)
            xt = _outproj(o, xt, mod[l], gains, attn_w_o[j], l, seq, "attn_out")
        xt = _ffn(xt, mod[l], gains, w_ffn_in, w_ffn_out, l, 1, 2, seq)
    return xt.reshape(batch, seq, d)
```

```python
import functools
import math

import jax
import jax.numpy as jnp
from jax import lax
from jax.experimental import pallas as pl
from jax.experimental.pallas import tpu as pltpu

F32 = jnp.float32
BF16 = jnp.bfloat16

NORM_EPS = 1e-6
CHUNK = 64
LRU_C = 8.0
N_HEADS = 8
N_SUB = 3
MASK_VALUE = -1e30

V7X_VMEM_BYTES = 64 * 1024 * 1024
VMEM_LIMIT_BYTES = V7X_VMEM_BYTES - 8 * 1024 * 1024
SUBLANES = 8
LANES = 128
ROW_CHUNK = 128
KEY_TILE = 512


def _params(*sem):
    return pltpu.CompilerParams(dimension_semantics=sem, vmem_limit_bytes=VMEM_LIMIT_BYTES)


def _dot(a, b):
    return jnp.dot(a, b, preferred_element_type=F32)


def _dot_nt(a, b):
    return lax.dot_general(a, b, (((1,), (1,)), ((), ())), preferred_element_type=F32)


def _row_chunks(n_rows, fn):
    rc = min(ROW_CHUNK, n_rows)

    def body(r, carry):
        fn(pl.ds(pl.multiple_of(r * rc, rc), rc))
        return carry

    lax.fori_loop(0, n_rows // rc, body, 0)


def _sublayer_vectors(vec_ref, mod_ref, g_ref, weight=None):
    vec_ref[0:1, :] = g_ref[0:1, :] * (1.0 + mod_ref[1:2, :])
    vec_ref[1:2, :] = mod_ref[0:1, :]
    if weight is not None:
        vec_ref[2:3, :] = (weight * mod_ref[2:3, :]) * g_ref[1:2, :]


def _row_rstd_to(stat_ref, src_ref):
    def fn(rows):
        x = src_ref[rows, :]
        ms = jnp.mean(x * x, axis=-1, keepdims=True)
        stat_ref[rows, :] = jnp.broadcast_to(lax.rsqrt(ms + NORM_EPS), (x.shape[0], LANES))

    _row_chunks(src_ref.shape[0], fn)


def _modulated_norm_to(h_ref, x_ref, stat_ref, vec_ref):
    _row_rstd_to(stat_ref, x_ref)

    def fn(rows):
        rstd = stat_ref[rows, :]
        for cb in range(x_ref.shape[1] // LANES):
            cols = slice(cb * LANES, (cb + 1) * LANES)
            h = x_ref[rows, cols] * rstd * vec_ref[0:1, cols] + vec_ref[1:2, cols]
            h_ref[rows, cols] = h.astype(BF16)

    _row_chunks(x_ref.shape[0], fn)


def _residual_epilogue(o_ref, x_ref, stat_ref, vec_ref):
    _row_rstd_to(stat_ref, o_ref)

    def fn(rows):
        rstd = stat_ref[rows, :]
        for cb in range(o_ref.shape[1] // LANES):
            cols = slice(cb * LANES, (cb + 1) * LANES)
            o_ref[rows, cols] = x_ref[rows, cols] + (o_ref[rows, cols] * rstd) * vec_ref[2:3, cols]

    _row_chunks(o_ref.shape[0], fn)


def _norm_scratch(tm, d, with_h):
    h = [pltpu.VMEM((tm, d), BF16)] if with_h else []
    return h + [pltpu.VMEM((tm, LANES), F32), pltpu.VMEM((SUBLANES, d), F32)]


def _modvec_kernel(c_ref, w_ref, b_ref, o_ref):
    c = c_ref[...]
    c_act = (c * jax.nn.sigmoid(c)).astype(BF16)
    o_ref[...] = _dot(c_act, w_ref[...].astype(BF16)) + b_ref[...]


def _modvec(c_pad, w, b, tn=1024):
    nl, d, n = w.shape
    tn = min(tn, n)
    return pl.pallas_call(
        _modvec_kernel,
        grid=(nl, n // tn),
        in_specs=[
            pl.BlockSpec((SUBLANES, d), lambda l, j: (0, 0)),
            pl.BlockSpec((None, d, tn), lambda l, j: (l, 0, j)),
            pl.BlockSpec((None, 1, tn), lambda l, j: (l, 0, j)),
        ],
        out_specs=pl.BlockSpec((None, SUBLANES, tn), lambda l, j: (l, 0, j)),
        out_shape=jax.ShapeDtypeStruct((nl, SUBLANES, n), F32),
        compiler_params=_params("arbitrary", "arbitrary"),
        name="modvec",
    )(c_pad, w, b.reshape(nl, 1, n))


def _ffn_kernel(x_ref, mod_ref, g_ref, wg_ref, wu_ref, wo_ref, o_ref, h_ref, stat_ref, vec_ref):
    j = pl.program_id(1)

    @pl.when(j == 0)
    def _():
        _sublayer_vectors(vec_ref, mod_ref, g_ref, 0.5)
        _modulated_norm_to(h_ref, x_ref, stat_ref, vec_ref)
        o_ref[...] = jnp.zeros(o_ref.shape, F32)

    h = h_ref[...]
    g = _dot(h, wg_ref[...].astype(BF16))
    u = _dot(h, wu_ref[...].astype(BF16))
    act = (g * jax.nn.sigmoid(g) * u).astype(BF16)
    o_ref[...] += _dot(act, wo_ref[...].astype(BF16))

    @pl.when(j == pl.num_programs(1) - 1)
    def _():
        _residual_epilogue(o_ref, x_ref, stat_ref, vec_ref)


def _ffn(x, mod, gains, w_in, w_out, layer, half, sub, seq, tm=1024, tf=256):
    t, d = x.shape
    f = w_out.shape[2]
    tm = min(tm, seq)
    tf = min(tf, f)
    per_batch = seq // tm
    nf = f // tf
    return pl.pallas_call(
        _ffn_kernel,
        grid=(t // tm, nf),
        in_specs=[
            pl.BlockSpec((tm, d), lambda i, j: (i, 0)),
            pl.BlockSpec((None, None, 3, d), lambda i, j: (i // per_batch, sub, 0, 0)),
            pl.BlockSpec((None, None, 2, d), lambda i, j: (layer, sub, 0, 0)),
            pl.BlockSpec((None, None, d, tf), lambda i, j: (layer, half, 0, j)),
            pl.BlockSpec((None, None, d, tf), lambda i, j: (layer, half, 0, j + nf)),
            pl.BlockSpec((None, None, tf, d), lambda i, j: (layer, half, j, 0)),
        ],
        out_specs=pl.BlockSpec((tm, d), lambda i, j: (i, 0)),
        out_shape=jax.ShapeDtypeStruct((t, d), F32),
        scratch_shapes=_norm_scratch(tm, d, with_h=True),
        compiler_params=_params("arbitrary", "arbitrary"),
        name=f"ffn_l{layer}h{half}",
    )(x, mod, gains, w_in, w_in, w_out)


def _proj_kernel(x_ref, mod_ref, g_ref, w_ref, o_ref, h_ref, stat_ref, vec_ref, *, out_scale, key_tile):
    @pl.when(pl.program_id(1) == 0)
    def _():
        _sublayer_vectors(vec_ref, mod_ref, g_ref)
        _modulated_norm_to(h_ref, x_ref, stat_ref, vec_ref)

    y = _dot(h_ref[...], w_ref[...].astype(BF16))
    if out_scale is not None:
        y = y * out_scale
    if key_tile is None:
        o_ref[...] = y.astype(o_ref.dtype)
    else:
        yt = y.T
        for s in range(o_ref.shape[0]):
            o_ref[s] = yt[:, s * key_tile:(s + 1) * key_tile].astype(o_ref.dtype)


def _proj(x, mod, mod_idx, gains, gain_idx, w, seq, out_dtype, name, col0=0, n=None, out_scale=None,
          key_tile=None, tm=1024, tn=1024):
    t, d = x.shape
    n = w.shape[1] if n is None else n
    tm = min(tm, seq)
    tn = min(tn, n)
    per_batch = seq // tm
    mr, gr = mod.shape[2], gains.shape[2]
    gi0, gi1 = gain_idx
    j0 = col0 // tn
    if key_tile is None:
        out_spec = pl.BlockSpec((tm, tn), lambda i, j: (i, j))
        out_shape = jax.ShapeDtypeStruct((t, n), out_dtype)
    else:
        key_tile = min(key_tile, tm)
        out_spec = pl.BlockSpec((tm // key_tile, tn, key_tile), lambda i, j: (i, j, 0))
        out_shape = jax.ShapeDtypeStruct((t // key_tile, n, key_tile), out_dtype)
    return pl.pallas_call(
        functools.partial(_proj_kernel, out_scale=out_scale, key_tile=key_tile),
        grid=(t // tm, n // tn),
        in_specs=[
            pl.BlockSpec((tm, d), lambda i, j: (i, 0)),
            pl.BlockSpec((None, None, mr, d), lambda i, j: (i // per_batch, mod_idx, 0, 0)),
            pl.BlockSpec((None, None, gr, d), lambda i, j: (gi0, gi1, 0, 0)),
            pl.BlockSpec((d, tn), lambda i, j: (0, j0 + j)),
        ],
        out_specs=out_spec,
        out_shape=out_shape,
        scratch_shapes=_norm_scratch(tm, d, with_h=True),
        compiler_params=_params("arbitrary", "arbitrary"),
        name=name,
    )(x, mod, gains, w)


def _outproj_kernel(a_ref, x_ref, mod_ref, g_ref, w_ref, o_ref, stat_ref, vec_ref):
    k = pl.program_id(1)

    @pl.when(k == 0)
    def _():
        o_ref[...] = jnp.zeros(o_ref.shape, F32)

    o_ref[...] += _dot(a_ref[...], w_ref[...].astype(BF16))

    @pl.when(k == pl.num_programs(1) - 1)
    def _():
        _sublayer_vectors(vec_ref, mod_ref, g_ref, 1.0)
        _residual_epilogue(o_ref, x_ref, stat_ref, vec_ref)


def _outproj(a, x, mod, gains, w, layer, seq, name, tm=1024, tk=512):
    t, d = x.shape
    kdim = a.shape[1]
    tm = min(tm, seq)
    tk = min(tk, kdim)
    per_batch = seq // tm
    return pl.pallas_call(
        _outproj_kernel,
        grid=(t // tm, kdim // tk),
        in_specs=[
            pl.BlockSpec((tm, tk), lambda i, k: (i, k)),
            pl.BlockSpec((tm, d), lambda i, k: (i, 0)),
            pl.BlockSpec((None, None, 3, d), lambda i, k: (i // per_batch, 1, 0, 0)),
            pl.BlockSpec((None, None, 2, d), lambda i, k: (layer, 1, 0, 0)),
            pl.BlockSpec((tk, d), lambda i, k: (k, 0)),
        ],
        out_specs=pl.BlockSpec((tm, d), lambda i, k: (i, 0)),
        out_shape=jax.ShapeDtypeStruct((t, d), F32),
        scratch_shapes=_norm_scratch(tm, d, with_h=False),
        compiler_params=_params("arbitrary", "arbitrary"),
        name=name,
    )(a, x, mod, gains, w)


def _gelu_tanh(x):
    cdf = 0.5 * (1.0 + jnp.tanh(math.sqrt(2.0 / math.pi) * (x + 0.044715 * (x * x * x))))
    return x * cdf


def _softplus(z):
    return jnp.maximum(z, 0.0) + jnp.log1p(jnp.exp(-jnp.abs(z)))


def _rglru_kernel(y_ref, xr_ref, cw_ref, cb_ref, wa_ref, wx_ref, ba_ref, bx_ref, lam_ref, z_ref,
                  xbuf, a_s, u_s, carry):
    tc, c = xr_ref.shape
    halo = SUBLANES
    kw = cw_ref.shape[0]

    @pl.when(pl.program_id(2) == 0)
    def _():
        xbuf[0:halo, :] = jnp.zeros((halo, c), F32)
        carry[...] = jnp.zeros((SUBLANES, c), F32)

    xbuf[halo:halo + tc, :] = xr_ref[...]
    xc = cb_ref[...]
    for k in range(kw):
        off = halo - (kw - 1) + k
        xc = xc + xbuf[off:off + tc, :] * cw_ref[k:k + 1, :]
    xbuf[0:halo, :] = xbuf[tc:tc + halo, :]

    xcb = xc.astype(BF16)
    r = jax.nn.sigmoid(_dot(xcb, wa_ref[...].astype(BF16)) + ba_ref[...])
    i = jax.nn.sigmoid(_dot(xcb, wx_ref[...].astype(BF16)) + bx_ref[...])
    log_a = (-LRU_C * r) * _softplus(-lam_ref[...])
    a = jnp.exp(log_a)
    a_s[...] = a
    u_s[...] = jnp.sqrt(-jnp.tanh(log_a) * (a * a + 1.0)) * (i * xc)

    row = lax.broadcasted_iota(jnp.int32, (SUBLANES, c), 0)

    def body(g, h_prev):
        rows = pl.ds(pl.multiple_of(g * SUBLANES, SUBLANES), SUBLANES)
        a = a_s[rows, :]
        u = u_s[rows, :]
        for d in (1, 2, 4):
            keep = row >= d
            u_shift = jnp.where(keep, pltpu.roll(u, d, 0), 0.0)
            a_shift = jnp.where(keep, pltpu.roll(a, d, 0), 1.0)
            u = u + a * u_shift
            a = a * a_shift
        h = u + a * h_prev
        u_s[rows, :] = h
        return jnp.broadcast_to(h[SUBLANES - 1:SUBLANES, :], (SUBLANES, c))

    carry[...] = lax.fori_loop(0, tc // SUBLANES, body, carry[...], unroll=8)
    z_ref[...] = (_gelu_tanh(y_ref[...]) * u_s[...]).astype(BF16)


def _rglru(yx, conv_w, conv_b, w_a, b_a, w_x, b_x, lam, batch, seq, tc=1024):
    t = yx.shape[0]
    nh, cb, _ = w_a.shape
    c = nh * cb
    tc = min(tc, seq)
    nt = seq // tc
    kw = conv_w.shape[0]
    row = lambda v: v.reshape(1, c)
    vec_spec = pl.BlockSpec((1, cb), lambda b, h, s: (0, h))
    return pl.pallas_call(
        _rglru_kernel,
        grid=(batch, nh, nt),
        in_specs=[
            pl.BlockSpec((tc, cb), lambda b, h, s: (b * nt + s, h)),
            pl.BlockSpec((tc, cb), lambda b, h, s: (b * nt + s, nh + h)),
            pl.BlockSpec((kw, cb), lambda b, h, s: (0, h)),
            vec_spec,
            pl.BlockSpec((None, cb, cb), lambda b, h, s: (h, 0, 0)),
            pl.BlockSpec((None, cb, cb), lambda b, h, s: (h, 0, 0)),
            vec_spec, vec_spec, vec_spec,
        ],
        out_specs=pl.BlockSpec((tc, cb), lambda b, h, s: (b * nt + s, h)),
        out_shape=jax.ShapeDtypeStruct((t, c), BF16),
        scratch_shapes=[
            pltpu.VMEM((tc + SUBLANES, cb), F32),
            pltpu.VMEM((tc, cb), F32),
            pltpu.VMEM((tc, cb), F32),
            pltpu.VMEM((SUBLANES, cb), F32),
        ],
        compiler_params=_params("arbitrary", "arbitrary", "arbitrary"),
        name="rglru",
    )(yx, yx, conv_w, row(conv_b), w_a, w_x, row(b_a), row(b_x), row(lam))


def _flash_kernel(q_ref, k_ref, vt_ref, lam_ref, gs_ref, o_ref, s_s, mb_s, m_s, l_s, acc_s, *, lam_init):
    tq, dq = q_ref.shape
    hd = dq // 2
    tk = vt_ref.shape[2]
    i = pl.program_id(2)

    m_s[...] = jnp.full(m_s.shape, MASK_VALUE, F32)
    l_s[...] = jnp.zeros(l_s.shape, F32)
    acc_s[...] = jnp.zeros(acc_s.shape, F32)

    def scores(j, masked):
        slot = j % 2
        k = k_ref[pl.ds(pl.multiple_of(j * tk, tk), tk), :]
        for c in range(2):
            q_c = q_ref[:, c * hd:(c + 1) * hd]
            k_c = k[:, c * hd:(c + 1) * hd]
            st = _dot_nt(k_c, q_c)
            if masked:
                k_chunk = lax.broadcasted_iota(jnp.int32, (tk, tq), 0) // CHUNK
                q_chunk = lax.broadcasted_iota(jnp.int32, (tk, tq), 1) // CHUNK
                st = jnp.where(k_chunk <= q_chunk, st, MASK_VALUE)
            s_s[slot, c] = st
            mb_s[slot, c] = jnp.max(st, axis=0, keepdims=True)

    def consume(j):
        slot = j % 2
        vt = vt_ref[j]
        for c in range(2):
            m_prev = m_s[c]
            m_new = jnp.maximum(m_prev, mb_s[slot, c])
            alpha = jnp.exp2(m_prev - m_new)
            p = jnp.exp2(s_s[slot, c] - m_new)
            l_s[c] = alpha * l_s[c] + jnp.sum(p, axis=0, keepdims=True)
            acc_s[c] = alpha * acc_s[c] + _dot(vt, p.astype(BF16))
            m_s[c] = m_new

    @pl.when(i == 0)
    def _():
        scores(0, True)

    @pl.when(i > 0)
    def _():
        scores(0, False)

    def body(j, carry):
        consume(j)
        scores(j + 1, False)
        return carry

    lax.fori_loop(0, i - 1, body, 0)

    @pl.when(i > 0)
    def _():
        consume(i - 1)
        scores(i, True)

    consume(i)

    lq = lam_ref[...]
    lam = (jnp.exp(jnp.sum(lq[0:1, :] * lq[1:2, :], axis=-1, keepdims=True))
           - jnp.exp(jnp.sum(lq[2:3, :] * lq[3:4, :], axis=-1, keepdims=True)) + lam_init)
    ot = acc_s[0] * (1.0 / l_s[0]) - lam * (acc_s[1] * (1.0 / l_s[1]))
    ot = ot * lax.rsqrt(jnp.mean(ot * ot, axis=0, keepdims=True) + NORM_EPS)
    o_ref[...] = (ot.T * gs_ref[...] * (1.0 - lam_init)).astype(o_ref.dtype)


def _flash(q, k, vt, lam_qk, g_sub, lam_init, batch, seq, tq=512):
    t, dq_all = q.shape
    dq = dq_all // N_HEADS
    _, dv_all, tk = vt.shape
    dv = dv_all // N_HEADS
    tq = min(tq, seq)
    assert tq == tk, "the diagonal-tile mask assumes square score tiles"
    nq = seq // tq
    kt_per_batch = seq // tk
    return pl.pallas_call(
        functools.partial(_flash_kernel, lam_init=lam_init),
        grid=(batch, N_HEADS, nq),
        in_specs=[
            pl.BlockSpec((tq, dq), lambda b, h, i: (b * nq + i, h)),
            pl.BlockSpec((seq, dq), lambda b, h, i: (b, h)),
            pl.BlockSpec((kt_per_batch, dv, tk), lambda b, h, i: (b, h, 0)),
            pl.BlockSpec(lam_qk.shape, lambda b, h, i: (0, 0)),
            pl.BlockSpec((1, dv), lambda b, h, i: (0, 0)),
        ],
        out_specs=pl.BlockSpec((tq, dv), lambda b, h, i: (b * nq + i, h)),
        out_shape=jax.ShapeDtypeStruct((t, N_HEADS * dv), BF16),
        scratch_shapes=[
            pltpu.VMEM((2, 2, tk, tq), F32),
            pltpu.VMEM((2, 2, 1, tq), F32),
            pltpu.VMEM((2, 1, tq), F32),
            pltpu.VMEM((2, 1, tq), F32),
            pltpu.VMEM((2, dv, tq), F32),
        ],
        compiler_params=_params("arbitrary", "arbitrary", "arbitrary"),
        name="flash",
    )(q, k, vt, lam_qk, g_sub.reshape(1, dv))


def kernel(x, c, w_mod, b_mod, norm_gain, w_ffn_in, w_ffn_out, lru_w_in, lru_conv_w, lru_conv_b, lru_w_a, lru_b_a, lru_w_x, lru_b_x, lru_lambda, lru_w_out, kv_gain, kv_w_mod, kv_b_mod, w_kv, attn_w_q, attn_lambda_qk, attn_sub_gain, attn_w_o):
    batch, seq, d = x.shape
    depth = w_mod.shape[0]
    n_a = lru_w_in.shape[0]
    assert batch <= SUBLANES and seq % CHUNK == 0

    c_pad = jnp.pad(c, ((0, SUBLANES - batch), (0, 0)))
    mod = _modvec(c_pad, w_mod, b_mod)[:, :batch].reshape(depth, batch, N_SUB, 3, d)
    kv_mod = _modvec(c_pad, kv_w_mod[None], kv_b_mod[None])[:, :batch].reshape(batch, 1, 2, d)
    gains = norm_gain.reshape(depth, N_SUB, 2, d)
    kv_gains = kv_gain.reshape(1, 1, 1, d)

    xt = x.reshape(batch * seq, d)
    hd = attn_w_q.shape[2] // (2 * N_HEADS)
    q_scale = hd ** -0.5 * math.log2(math.e)
    n_k = 2 * hd * N_HEADS
    k = vt = None
    for l in range(depth):
        if l == n_a:
            k = _proj(xt, kv_mod, 0, kv_gains, (0, 0), w_kv, seq, BF16, "k_proj", col0=0, n=n_k)
            vt = _proj(xt, kv_mod, 0, kv_gains, (0, 0), w_kv, seq, BF16, "v_proj", col0=n_k,
                       n=w_kv.shape[1] - n_k, key_tile=KEY_TILE)
        xt = _ffn(xt, mod[l], gains, w_ffn_in, w_ffn_out, l, 0, 0, seq)
        if l < n_a:
            yx = _proj(xt, mod[l], 1, gains, (l, 1), lru_w_in[l], seq, F32, "lru_in")
            z = _rglru(yx, lru_conv_w[l], lru_conv_b[l], lru_w_a[l], lru_b_a[l], lru_w_x[l], lru_b_x[l],
                       lru_lambda[l], batch, seq)
            xt = _outproj(z, xt, mod[l], gains, lru_w_out[l], l, seq, "lru_out")
        else:
            j = l - n_a
            lam_init = 0.8 - 0.6 * math.exp(-0.3 * l)
            q = _proj(xt, mod[l], 1, gains, (l, 1), attn_w_q[j], seq, BF16, "q_proj", out_scale=q_scale)
            o = _flash(q, k, vt, attn_lambda_qk[j], attn_sub_gain[j], lam_init, batch, seq)
            xt = _outproj(o, xt, mod[l], gains, attn_w_o[j], l, seq, "attn_out")
        xt = _ffn(xt, mod[l], gains, w_ffn_in, w_ffn_out, l, 1, 2, seq)
    return xt.reshape(batch, seq, d)
```
